```python
import jax, jax.numpy as jnp
from jax import lax
import numpy as np

D_MODEL = 1024
BATCH = 2
SEQ = 8192
DEPTH = 1

HEAD_DIM = D_MODEL // 16
ATTN_HEADS = 8
ATTN_KV_HEADS = 2
ATTN_GROUP = ATTN_HEADS // ATTN_KV_HEADS
ATTN_WIDTH = ATTN_HEADS * HEAD_DIM
KV_WIDTH = ATTN_KV_HEADS * HEAD_DIM
WINDOW = 128
BLOCK = 128
ROPE_THETA = 10000.0
POOL_WINDOWS = (2, 4, 8, 16)
POOL_GROUPS = 4
POOL_WIDTH = D_MODEL - ATTN_WIDTH
POOL_CH = POOL_WIDTH // POOL_GROUPS
MIX_WIDTH = ATTN_WIDTH + POOL_WIDTH
IN_WIDTH = ATTN_WIDTH + 2 * KV_WIDTH + POOL_WIDTH
PEER_HEADS = 8
PEER_N_KEYS = 128
PEER_N_EXPERTS = PEER_N_KEYS * PEER_N_KEYS
PEER_TOPK = 16
PEER_QDIM = 256
PEER_HALF = PEER_QDIM // 2
PEER_CHUNK = 128
DEEPNORM_ALPHA = (2.0 * DEPTH) ** 0.25
DEEPNORM_BETA = (8.0 * DEPTH) ** -0.25
LN_EPS = 1e-5

kernel_name = "hybrid_swa_sink_pool_peer_deepnorm_adaln"


def layer_norm(x, g, b):
    x32 = x.astype(jnp.float32)
    mu = jnp.mean(x32, axis=-1, keepdims=True)
    var = jnp.mean(jnp.square(x32 - mu), axis=-1, keepdims=True)
    return ((x32 - mu) * lax.rsqrt(var + LN_EPS) * g + b).astype(x.dtype)


def rope(t, pos):
    half = t.shape[-1] // 2
    freqs = ROPE_THETA ** (-jnp.arange(half, dtype=jnp.float32) / half)
    ang = pos.astype(jnp.float32)[:, None] * freqs[None, :]
    cos = jnp.cos(ang)[None, :, None, :]
    sin = jnp.sin(ang)[None, :, None, :]
    t32 = t.astype(jnp.float32)
    t1, t2 = t32[..., :half], t32[..., half:]
    return jnp.concatenate([t1 * cos - t2 * sin, t2 * cos + t1 * sin], axis=-1).astype(t.dtype)


def sliding_window_gqa_with_sinks(q, k, v, sinks):
    B, S = q.shape[0], q.shape[1]
    nb = S // BLOCK
    qb = q.reshape(B, nb, BLOCK, ATTN_KV_HEADS, ATTN_GROUP, HEAD_DIM)
    kb = k.reshape(B, nb, BLOCK, ATTN_KV_HEADS, HEAD_DIM)
    vb = v.reshape(B, nb, BLOCK, ATTN_KV_HEADS, HEAD_DIM)
    kcat = jnp.concatenate([jnp.concatenate([jnp.zeros_like(kb[:, :1]), kb[:, :-1]], axis=1), kb], axis=2)
    vcat = jnp.concatenate([jnp.concatenate([jnp.zeros_like(vb[:, :1]), vb[:, :-1]], axis=1), vb], axis=2)
    s = jnp.einsum('bnqhgd,bnshd->bnhgqs', qb, kcat).astype(jnp.float32) * (HEAD_DIM ** -0.5)
    q_idx = jnp.arange(BLOCK)[:, None]
    s_idx = jnp.arange(2 * BLOCK)[None, :]
    rel = q_idx + BLOCK - s_idx
    band = (rel >= 0) & (rel < WINDOW)
    blk = jnp.arange(nb)[:, None, None]
    valid = band[None] & ((blk > 0) | (s_idx[None] >= BLOCK))
    s = jnp.where(valid[None, :, None, None], s, -1e30)
    sink_b = sinks.astype(jnp.float32).reshape(1, 1, ATTN_KV_HEADS, ATTN_GROUP, 1, 1)
    m = jnp.maximum(jnp.max(s, axis=-1, keepdims=True), sink_b)
    p = jnp.exp(s - m)
    denom = jnp.sum(p, axis=-1, keepdims=True) + jnp.exp(sink_b - m)
    probs = (p / denom).astype(v.dtype)
    o = jnp.einsum('bnhgqs,bnshd->bnqhgd', probs, vcat)
    return o.reshape(B, S, ATTN_WIDTH)


def multiscale_causal_pool(p, w_pool, pool_scale):
    B, S, _ = p.shape
    p32 = p.astype(jnp.float32).reshape(B, S, POOL_GROUPS, POOL_CH)
    cs = jnp.cumsum(p32, axis=1)
    t = jnp.arange(S)
    outs = []
    for gi, w in enumerate(POOL_WINDOWS):
        csg = cs[:, :, gi]
        lag = jnp.pad(csg, ((0, 0), (w, 0), (0, 0)))[:, :S]
        cnt = jnp.minimum(t + 1, w).astype(jnp.float32)[None, :, None]
        outs.append((csg - lag) / cnt - p32[:, :, gi])
    pooled = jnp.stack(outs, axis=2).astype(p.dtype)
    y = jnp.einsum('bsgc,gce->bsge', pooled, w_pool).reshape(B, S, POOL_WIDTH)
    return y * pool_scale


def peer_ffn(h, w_pq, peer_k1, peer_k2, peer_u, peer_v):
    B, S, D = h.shape
    q = (h @ w_pq).reshape(B, S, PEER_HEADS, 2, PEER_HALF)
    s1 = jnp.einsum('bshc,kc->bshk', q[..., 0, :], peer_k1).astype(jnp.float32)
    s2 = jnp.einsum('bshc,kc->bshk', q[..., 1, :], peer_k2).astype(jnp.float32)
    v1, i1 = lax.top_k(s1, PEER_TOPK)
    v2, i2 = lax.top_k(s2, PEER_TOPK)
    comb = (v1[..., :, None] + v2[..., None, :]).reshape(B, S, PEER_HEADS, PEER_TOPK * PEER_TOPK)
    vals, flat = lax.top_k(comb, PEER_TOPK)
    e = (jnp.take_along_axis(i1, flat // PEER_TOPK, axis=-1) * PEER_N_KEYS
         + jnp.take_along_axis(i2, flat % PEER_TOPK, axis=-1))
    g = jax.nn.softmax(vals, axis=-1).astype(h.dtype)
    n_chunks = (B * S) // PEER_CHUNK
    xs = h.reshape(n_chunks, PEER_CHUNK, D)
    es = e.reshape(n_chunks, PEER_CHUNK, PEER_HEADS * PEER_TOPK)
    gs = g.reshape(n_chunks, PEER_CHUNK, PEER_HEADS * PEER_TOPK)

    def expert_chunk(args):
        xc, ec, gc = args
        hid = jnp.einsum('td,tkd->tk', xc, peer_u[ec])
        act = jax.nn.gelu(hid, approximate=False) * gc
        return jnp.einsum('tk,tkd->td', act, peer_v[ec])

    out = lax.map(expert_chunk, (xs, es, gs))
    return out.reshape(B, S, D)


def setup_inputs(seed: int = 0) -> dict:
    key = jax.random.key(seed)
    ks = jax.random.split(key, 20)

    def nrm(k, shape, s):
        return jax.random.normal(k, shape, jnp.float32) * s

    col_scale = jnp.concatenate([
        jnp.ones((ATTN_WIDTH + KV_WIDTH,), jnp.float32),
        jnp.full((KV_WIDTH + POOL_WIDTH,), DEEPNORM_BETA, jnp.float32)])
    return {
        "x": nrm(ks[0], (BATCH, SEQ, D_MODEL), 1.0),
        "c": nrm(ks[1], (BATCH, D_MODEL), 1.0),
        "w_ada": nrm(ks[2], (DEPTH, D_MODEL, 6 * D_MODEL), 0.1 * D_MODEL ** -0.5),
        "b_ada": nrm(ks[3], (DEPTH, 6 * D_MODEL), 0.02),
        "w_in": nrm(ks[4], (DEPTH, D_MODEL, IN_WIDTH), D_MODEL ** -0.5) * col_scale,
        "attn_sinks": nrm(ks[5], (DEPTH, ATTN_HEADS), 1.0),
        "w_pool": nrm(ks[6], (DEPTH, POOL_GROUPS, POOL_CH, POOL_CH), POOL_CH ** -0.5),
        "pool_scale": 1.0 + nrm(ks[7], (DEPTH, POOL_WIDTH), 0.02),
        "w_out": nrm(ks[8], (DEPTH, MIX_WIDTH, D_MODEL), DEEPNORM_BETA * MIX_WIDTH ** -0.5),
        "ln1_g": 1.0 + nrm(ks[9], (DEPTH, D_MODEL), 0.02),
        "ln1_b": nrm(ks[10], (DEPTH, D_MODEL), 0.02),
        "w_pq": nrm(ks[11], (DEPTH, D_MODEL, PEER_HEADS * PEER_QDIM), D_MODEL ** -0.5),
        "peer_k1": nrm(ks[12], (DEPTH, PEER_N_KEYS, PEER_HALF), PEER_HALF ** -0.5),
        "peer_k2": nrm(ks[13], (DEPTH, PEER_N_KEYS, PEER_HALF), PEER_HALF ** -0.5),
        "peer_u": nrm(ks[14], (DEPTH, PEER_N_EXPERTS, D_MODEL), D_MODEL ** -0.5),
        "peer_v": nrm(ks[15], (DEPTH, PEER_N_EXPERTS, D_MODEL), DEEPNORM_BETA),
        "ln2_g": 1.0 + nrm(ks[16], (DEPTH, D_MODEL), 0.02),
        "ln2_b": nrm(ks[17], (DEPTH, D_MODEL), 0.02),
    }


def reference(x, c, w_ada, b_ada, w_in, attn_sinks, w_pool, pool_scale, w_out, ln1_g, ln1_b,
              w_pq, peer_k1, peer_k2, peer_u, peer_v, ln2_g, ln2_b):
    B, S, _ = x.shape
    pos = jnp.arange(S, dtype=jnp.int32)
    for l in range(DEPTH):
        mod = jax.nn.silu(c) @ w_ada[l] + b_ada[l]
        shift1, scale1, gate1, shift2, scale2, gate2 = [m[:, None, :] for m in jnp.split(mod, 6, axis=-1)]
        h = x * (1.0 + scale1) + shift1
        proj = h @ w_in[l]
        q = proj[..., :ATTN_WIDTH].reshape(B, S, ATTN_HEADS, HEAD_DIM)
        k = proj[..., ATTN_WIDTH:ATTN_WIDTH + KV_WIDTH].reshape(B, S, ATTN_KV_HEADS, HEAD_DIM)
        v = proj[..., ATTN_WIDTH + KV_WIDTH:ATTN_WIDTH + 2 * KV_WIDTH].reshape(B, S, ATTN_KV_HEADS, HEAD_DIM)
        p_in = proj[..., ATTN_WIDTH + 2 * KV_WIDTH:]
        attn = sliding_window_gqa_with_sinks(rope(q, pos), rope(k, pos), v, attn_sinks[l])
        pool = multiscale_causal_pool(p_in, w_pool[l], pool_scale[l])
        mix = jnp.concatenate([attn, pool], axis=-1) @ w_out[l]
        x = layer_norm(DEEPNORM_ALPHA * x + (1.0 + gate1) * mix, ln1_g[l], ln1_b[l])
        h2 = x * (1.0 + scale2) + shift2
        y = peer_ffn(h2, w_pq[l], peer_k1[l], peer_k2[l], peer_u[l], peer_v[l])
        x = layer_norm(DEEPNORM_ALPHA * x + (1.0 + gate2) * y, ln2_g[l], ln2_b[l])
    return x
```

```python
import functools
import math

import jax
import jax.numpy as jnp
from jax import lax
from jax.experimental import pallas as pl
from jax.experimental.pallas import tpu as pltpu

D_MODEL = 1024
HEAD_DIM = 64
ATTN_HEADS = 8
ATTN_KV_HEADS = 2
ATTN_GROUP = ATTN_HEADS // ATTN_KV_HEADS
ATTN_WIDTH = ATTN_HEADS * HEAD_DIM
KV_WIDTH = ATTN_KV_HEADS * HEAD_DIM
QK_WIDTH = ATTN_WIDTH + KV_WIDTH
WINDOW = 128
BLOCK = 128
ROPE_THETA = 10000.0
POOL_WINDOWS = (2, 4, 8, 16)
POOL_GROUPS = 4
POOL_WIDTH = D_MODEL - ATTN_WIDTH
POOL_CH = POOL_WIDTH // POOL_GROUPS
POOL_HALO = 16
IN_WIDTH = ATTN_WIDTH + 2 * KV_WIDTH + POOL_WIDTH
PEER_HEADS = 8
PEER_N_KEYS = 128
PEER_N_EXPERTS = PEER_N_KEYS * PEER_N_KEYS
PEER_TOPK = 16
PEER_QDIM = 256
PEER_HALF = PEER_QDIM // 2
DEPTH = 1
DEEPNORM_ALPHA = (2.0 * DEPTH) ** 0.25
LN_EPS = 1e-5
NEG_BIG = -1e30

LANES = 128

PROJ_ROWS = 512
MIX_ROWS = 512
ROUTE_ROWS = 256
PEER_ROWS = 512
PEER_EXPERT_TILE = 1024
VMEM_LIMIT = 52 * 1024 * 1024

_NT = (((1,), (1,)), ((), ()))
_TN = (((0,), (0,)), ((), ()))


def _layer_norm(z, g, b):
    mu = jnp.mean(z, axis=-1, keepdims=True)
    zc = z - mu
    var = jnp.mean(zc * zc, axis=-1, keepdims=True)
    return zc * lax.rsqrt(var + LN_EPS) * g + b


def _mod_kernel(c_ref, w_ref, b_ref, o_ref):
    c = c_ref[...]
    s = c * jax.nn.sigmoid(c)
    o_ref[...] = jnp.dot(s.astype(jnp.bfloat16), w_ref[...].astype(jnp.bfloat16),
                         preferred_element_type=jnp.float32) + b_ref[...]


def _mod_call(c_pad, w_ada, b_ada):
    n_out = w_ada.shape[1]
    tile = 1024
    return pl.pallas_call(
        _mod_kernel,
        grid=(n_out // tile,),
        in_specs=[
            pl.BlockSpec(c_pad.shape, lambda j: (0, 0)),
            pl.BlockSpec((D_MODEL, tile), lambda j: (0, j)),
            pl.BlockSpec((1, tile), lambda j: (0, j)),
        ],
        out_specs=pl.BlockSpec((c_pad.shape[0], tile), lambda j: (0, j)),
        out_shape=jax.ShapeDtypeStruct((c_pad.shape[0], n_out), jnp.float32),
        compiler_params=pltpu.CompilerParams(dimension_semantics=("parallel",)),
        name="adaln_mod",
    )(c_pad, w_ada, b_ada)


def _proj_kernel(tiles_per_seq, x_ref, scale_ref, shift_ref, w_ref, freq_ref,
                 q_ref, k_ref, v_ref, p_ref):
    i = pl.program_id(0)
    rows = x_ref.shape[0]
    h = x_ref[...] * (1.0 + scale_ref[0]) + shift_ref[0]
    proj = jnp.dot(h.astype(jnp.bfloat16), w_ref[...], preferred_element_type=jnp.float32)

    pos0 = (i % tiles_per_seq) * rows
    pos = (pos0 + lax.broadcasted_iota(jnp.int32, (rows, LANES), 0)).astype(jnp.float32)
    ang = pos * freq_ref[...]
    cos = jnp.cos(ang)
    sin = jnp.sin(ang)
    lane = lax.broadcasted_iota(jnp.int32, (rows, LANES), 1)
    first_half = (lane % HEAD_DIM) < (HEAD_DIM // 2)
    sin_signed = jnp.where(first_half, -sin, sin)

    def rope_chunk(t):
        partner = jnp.where(first_half,
                            pltpu.roll(t, LANES - HEAD_DIM // 2, 1),
                            pltpu.roll(t, HEAD_DIM // 2, 1))
        return t * cos + partner * sin_signed

    q_scale = HEAD_DIM ** -0.5
    for ch in range(ATTN_WIDTH // LANES):
        t = proj[:, ch * LANES:(ch + 1) * LANES]
        q_ref[:, ch * LANES:(ch + 1) * LANES] = (rope_chunk(t) * q_scale).astype(q_ref.dtype)
    k_ref[...] = rope_chunk(proj[:, ATTN_WIDTH:QK_WIDTH]).astype(k_ref.dtype)
    v_ref[...] = proj[:, QK_WIDTH:QK_WIDTH + KV_WIDTH].astype(v_ref.dtype)
    p_ref[...] = proj[:, QK_WIDTH + KV_WIDTH:]


def _proj_call(x2d, scale1, shift1, w_in_bf, freq_row, seq):
    tokens = x2d.shape[0]
    rows = PROJ_ROWS
    tiles_per_seq = seq // rows
    mod_spec = pl.BlockSpec((1, 1, D_MODEL), lambda i: (i // tiles_per_seq, 0, 0))
    return pl.pallas_call(
        functools.partial(_proj_kernel, tiles_per_seq),
        grid=(tokens // rows,),
        in_specs=[
            pl.BlockSpec((rows, D_MODEL), lambda i: (i, 0)),
            mod_spec, mod_spec,
            pl.BlockSpec((D_MODEL, IN_WIDTH), lambda i: (0, 0)),
            pl.BlockSpec((1, LANES), lambda i: (0, 0)),
        ],
        out_specs=[
            pl.BlockSpec((rows, ATTN_WIDTH), lambda i: (i, 0)),
            pl.BlockSpec((rows, KV_WIDTH), lambda i: (i, 0)),
            pl.BlockSpec((rows, KV_WIDTH), lambda i: (i, 0)),
            pl.BlockSpec((rows, POOL_WIDTH), lambda i: (i, 0)),
        ],
        out_shape=[
            jax.ShapeDtypeStruct((tokens, ATTN_WIDTH), jnp.bfloat16),
            jax.ShapeDtypeStruct((tokens, KV_WIDTH), jnp.bfloat16),
            jax.ShapeDtypeStruct((tokens, KV_WIDTH), jnp.bfloat16),
            jax.ShapeDtypeStruct((tokens, POOL_WIDTH), jnp.float32),
        ],
        compiler_params=pltpu.CompilerParams(dimension_semantics=("parallel",),
                                             vmem_limit_bytes=VMEM_LIMIT),
        name="in_proj_rope",
    )(x2d, scale1, shift1, w_in_bf, freq_row)


def _mix_kernel(tiles_per_seq, sinks_ref, x_ref, q_ref, kc_ref, kp_ref, vc_ref, vp_ref,
                pc_ref, pp_ref, wpool_ref, pscale_ref, wout_ref, gate1_ref, scale2_ref,
                shift2_ref, g_ref, b_ref, x1_ref, h2_ref):
    i = pl.program_id(0)
    rows = x_ref.shape[0]
    seq_start = (i % tiles_per_seq) == 0

    kext = jnp.concatenate([kp_ref[...], kc_ref[...]], axis=0)
    vext = jnp.concatenate([vp_ref[...], vc_ref[...]], axis=0)
    r_idx = lax.broadcasted_iota(jnp.int32, (BLOCK, 2 * BLOCK), 0)
    c_idx = lax.broadcasted_iota(jnp.int32, (BLOCK, 2 * BLOCK), 1)
    band = (c_idx > r_idx) & (c_idx <= r_idx + WINDOW)
    first_key = jnp.where(seq_start, BLOCK, 0)
    attn_blocks = []
    for blk in range(rows // BLOCK):
        valid = (band & (c_idx >= first_key)) if blk == 0 else band
        valid4 = jnp.concatenate([valid] * ATTN_GROUP, axis=0)
        kcat = kext[blk * BLOCK:(blk + 2) * BLOCK]
        vcat = vext[blk * BLOCK:(blk + 2) * BLOCK]
        q_blk = q_ref[blk * BLOCK:(blk + 1) * BLOCK, :]
        head_out = []
        for g in range(ATTN_KV_HEADS):
            heads = range(g * ATTN_GROUP, (g + 1) * ATTN_GROUP)
            q4 = jnp.concatenate([q_blk[:, h * HEAD_DIM:(h + 1) * HEAD_DIM] for h in heads], axis=0)
            s = lax.dot_general(q4, kcat[:, g * HEAD_DIM:(g + 1) * HEAD_DIM], _NT,
                                preferred_element_type=jnp.float32)
            s = jnp.where(valid4, s, NEG_BIG)
            sink = jnp.concatenate(
                [jnp.full((BLOCK, 1), sinks_ref[h], jnp.float32) for h in heads], axis=0)
            m = jnp.maximum(jnp.max(s, axis=-1, keepdims=True), sink)
            p = jnp.exp(s - m)
            denom = jnp.sum(p, axis=-1, keepdims=True) + jnp.exp(sink - m)
            o = jnp.dot(p.astype(jnp.bfloat16), vcat[:, g * HEAD_DIM:(g + 1) * HEAD_DIM],
                        preferred_element_type=jnp.float32) / denom
            head_out += [o[k * BLOCK:(k + 1) * BLOCK] for k in range(ATTN_GROUP)]
        attn_blocks.append(jnp.concatenate(head_out, axis=1))
    attn = jnp.concatenate(attn_blocks, axis=0)

    halo = jnp.where(seq_start, 0.0, pp_ref[...])
    pext = jnp.concatenate([halo, pc_ref[...]], axis=0)
    pos = ((i % tiles_per_seq) * rows
           + lax.broadcasted_iota(jnp.int32, (rows, 1), 0))
    pool_parts = []
    for gi, w in enumerate(POOL_WINDOWS):
        cols = slice(gi * POOL_CH, (gi + 1) * POOL_CH)
        tot = pext[POOL_HALO:POOL_HALO + rows, cols]
        for d in range(1, w):
            tot = tot + pext[POOL_HALO - d:POOL_HALO - d + rows, cols]
        cnt = jnp.minimum(pos + 1, w).astype(jnp.float32)
        pooled = tot / cnt - pext[POOL_HALO:POOL_HALO + rows, cols]
        pool_parts.append(jnp.dot(pooled.astype(jnp.bfloat16), wpool_ref[gi],
                                  preferred_element_type=jnp.float32))
    pool = jnp.concatenate(pool_parts, axis=1) * pscale_ref[...]

    mix = (jnp.dot(attn.astype(jnp.bfloat16), wout_ref[:ATTN_WIDTH, :],
                   preferred_element_type=jnp.float32)
           + jnp.dot(pool.astype(jnp.bfloat16), wout_ref[ATTN_WIDTH:, :],
                     preferred_element_type=jnp.float32))
    z = DEEPNORM_ALPHA * x_ref[...] + (1.0 + gate1_ref[0]) * mix
    x1 = _layer_norm(z, g_ref[...], b_ref[...])
    x1_ref[...] = x1
    h2_ref[...] = (x1 * (1.0 + scale2_ref[0]) + shift2_ref[0]).astype(h2_ref.dtype)


def _mix_call(sinks, x2d, q, k, v, p, w_pool_bf, pool_scale, w_out_bf, gate1, scale2, shift2,
              ln_g, ln_b, seq):
    tokens = x2d.shape[0]
    rows = MIX_ROWS
    tiles_per_seq = seq // rows
    blocks_per_tile = rows // BLOCK
    halos_per_tile = rows // POOL_HALO
    mod_spec = pl.BlockSpec((1, 1, D_MODEL), lambda i, s: (i // tiles_per_seq, 0, 0))
    row_spec = lambda width: pl.BlockSpec((rows, width), lambda i, s: (i, 0))
    prev_block = lambda i, s: (jnp.maximum(i * blocks_per_tile - 1, 0), 0)
    prev_halo = lambda i, s: (jnp.maximum(i * halos_per_tile - 1, 0), 0)
    full = lambda shape: pl.BlockSpec(shape, lambda i, s: (0,) * len(shape))
    grid_spec = pltpu.PrefetchScalarGridSpec(
        num_scalar_prefetch=1,
        grid=(tokens // rows,),
        in_specs=[
            row_spec(D_MODEL),
            row_spec(ATTN_WIDTH),
            row_spec(KV_WIDTH),
            pl.BlockSpec((BLOCK, KV_WIDTH), prev_block),
            row_spec(KV_WIDTH),
            pl.BlockSpec((BLOCK, KV_WIDTH), prev_block),
            row_spec(POOL_WIDTH),
            pl.BlockSpec((POOL_HALO, POOL_WIDTH), prev_halo),
            full((POOL_GROUPS, POOL_CH, POOL_CH)),
            full((1, POOL_WIDTH)),
            full((D_MODEL, D_MODEL)),
            mod_spec, mod_spec, mod_spec,
            full((1, D_MODEL)), full((1, D_MODEL)),
        ],
        out_specs=[row_spec(D_MODEL), row_spec(D_MODEL)],
    )
    return pl.pallas_call(
        functools.partial(_mix_kernel, tiles_per_seq),
        grid_spec=grid_spec,
        out_shape=[
            jax.ShapeDtypeStruct((tokens, D_MODEL), jnp.float32),
            jax.ShapeDtypeStruct((tokens, D_MODEL), jnp.bfloat16),
        ],
        compiler_params=pltpu.CompilerParams(dimension_semantics=("parallel",),
                                             vmem_limit_bytes=VMEM_LIMIT),
        name="mixers_ln1",
    )(sinks, x2d, q, k, k, v, v, p, p, w_pool_bf, pool_scale, w_out_bf, gate1, scale2, shift2,
      ln_g, ln_b)


def _top_rows(s, count):
    out = []
    for _ in range(count):
        m = jnp.max(s, axis=0, keepdims=True)
        out.append(m)
        s = jnp.where(s >= m, -jnp.inf, s)
    return out


def _route_kernel(h2_ref, wpq_ref, k1_ref, k2_ref, thr_ref, ea_ref, s2_ref, eb_ref):
    q = jnp.dot(h2_ref[...], wpq_ref[...], preferred_element_type=jnp.float32)
    q = q.astype(jnp.bfloat16)
    half_k = PEER_TOPK // 2
    for h in range(PEER_HEADS):
        qa = q[:, h * PEER_QDIM:h * PEER_QDIM + PEER_HALF]
        qb = q[:, h * PEER_QDIM + PEER_HALF:(h + 1) * PEER_QDIM]
        s1 = lax.dot_general(k1_ref[...], qa, _NT, preferred_element_type=jnp.float32)
        s2 = lax.dot_general(k2_ref[...], qb, _NT, preferred_element_type=jnp.float32)
        a = _top_rows(s1, PEER_TOPK)
        b = _top_rows(s2, PEER_TOPK)
        b_lo = jnp.concatenate(b[:half_k], axis=0)
        cands = [a[r] + b_lo for r in range(half_k)]
        cands.append(a[0] + jnp.concatenate(b[half_k:], axis=0))
        cands.append(jnp.concatenate(a[half_k:], axis=0) + b[0])
        cand = jnp.concatenate(cands, axis=0)
        tau = _top_rows(cand, PEER_TOPK)[-1]
        top = a[0] + b[0]
        z = jnp.sum(jnp.where(cand >= tau, jnp.exp(cand - top), 0.0), axis=0, keepdims=True)
        thr_ref[h] = tau - s1
        ea_ref[h] = jnp.exp(s1 - a[0])
        s2_ref[h] = s2
        eb_ref[h] = jnp.exp(s2 - b[0]) / z


def _route_call(h2, w_pq_bf, k1_bf, k2_bf):
    tokens = h2.shape[0]
    rows = ROUTE_ROWS
    out_sds = jax.ShapeDtypeStruct((PEER_HEADS, PEER_N_KEYS, tokens), jnp.float32)
    out_spec = pl.BlockSpec((PEER_HEADS, PEER_N_KEYS, rows), lambda i: (0, 0, i))
    return pl.pallas_call(
        _route_kernel,
        grid=(tokens // rows,),
        in_specs=[
            pl.BlockSpec((rows, D_MODEL), lambda i: (i, 0)),
            pl.BlockSpec((D_MODEL, PEER_HEADS * PEER_QDIM), lambda i: (0, 0)),
            pl.BlockSpec((PEER_N_KEYS, PEER_HALF), lambda i: (0, 0)),
            pl.BlockSpec((PEER_N_KEYS, PEER_HALF), lambda i: (0, 0)),
        ],
        out_specs=[out_spec] * 4,
        out_shape=[out_sds] * 4,
        compiler_params=pltpu.CompilerParams(dimension_semantics=("parallel",),
                                             vmem_limit_bytes=VMEM_LIMIT),
        name="peer_route",
    )(h2, w_pq_bf, k1_bf, k2_bf)


def _peer_kernel(h2_ref, u_ref, v_ref, thr_ref, ea_ref, s2_ref, eb_ref, x1_ref, gate2_ref,
                 g_ref, b_ref, o_ref, acc_ref, hid_ref, act_ref):
    j = pl.program_id(1)
    rows = h2_ref.shape[0]
    rows_i = u_ref.shape[0] // PEER_N_KEYS

    @pl.when(j == 0)
    def _():
        acc_ref[...] = jnp.zeros_like(acc_ref)

    hid_ref[...] = lax.dot_general(u_ref[...], h2_ref[...], _NT,
                                   preferred_element_type=jnp.float32)
    sqrt_half = math.sqrt(0.5)
    for tb in range(rows // LANES):
        tok = slice(tb * LANES, (tb + 1) * LANES)
        for il in range(rows_i):
            gate = jnp.zeros((PEER_N_KEYS, LANES), jnp.float32)
            for h in range(PEER_HEADS):
                thr = thr_ref[h, il:il + 1, tok]
                ea = ea_ref[h, il:il + 1, tok]
                gate = gate + jnp.where(s2_ref[h, :, tok] >= thr, eb_ref[h, :, tok], 0.0) * ea
            hid = hid_ref[il * PEER_N_KEYS:(il + 1) * PEER_N_KEYS, tok]
            gelu = 0.5 * hid * (1.0 + lax.erf(hid * sqrt_half))
            act_ref[il * PEER_N_KEYS:(il + 1) * PEER_N_KEYS, tok] = (gelu * gate).astype(act_ref.dtype)

    acc_ref[...] += lax.dot_general(act_ref[...], v_ref[...], _TN,
                                    preferred_element_type=jnp.float32)

    @pl.when(j == pl.num_programs(1) - 1)
    def _():
        z = DEEPNORM_ALPHA * x1_ref[...] + (1.0 + gate2_ref[0]) * acc_ref[...]
        o_ref[...] = _layer_norm(z, g_ref[...], b_ref[...])


def _peer_call(h2, u_bf, v_bf, thr, ea, s2, eb, x1, gate2, ln_g, ln_b, seq):
    tokens = h2.shape[0]
    rows = PEER_ROWS
    et = PEER_EXPERT_TILE
    tiles_per_seq = seq // rows
    rows_i = et // PEER_N_KEYS
    sel_spec = pl.BlockSpec((PEER_HEADS, rows_i, rows), lambda i, j: (0, j, i))
    key_spec = pl.BlockSpec((PEER_HEADS, PEER_N_KEYS, rows), lambda i, j: (0, 0, i))
    return pl.pallas_call(
        _peer_kernel,
        grid=(tokens // rows, PEER_N_EXPERTS // et),
        in_specs=[
            pl.BlockSpec((rows, D_MODEL), lambda i, j: (i, 0)),
            pl.BlockSpec((et, D_MODEL), lambda i, j: (j, 0)),
            pl.BlockSpec((et, D_MODEL), lambda i, j: (j, 0)),
            sel_spec, sel_spec,
            key_spec, key_spec,
            pl.BlockSpec((rows, D_MODEL), lambda i, j: (i, 0)),
            pl.BlockSpec((1, 1, D_MODEL), lambda i, j: (i // tiles_per_seq, 0, 0)),
            pl.BlockSpec((1, D_MODEL), lambda i, j: (0, 0)),
            pl.BlockSpec((1, D_MODEL), lambda i, j: (0, 0)),
        ],
        out_specs=pl.BlockSpec((rows, D_MODEL), lambda i, j: (i, 0)),
        out_shape=jax.ShapeDtypeStruct((tokens, D_MODEL), jnp.float32),
        scratch_shapes=[
            pltpu.VMEM((rows, D_MODEL), jnp.float32),
            pltpu.VMEM((et, rows), jnp.float32),
            pltpu.VMEM((et, rows), jnp.bfloat16),
        ],
        compiler_params=pltpu.CompilerParams(dimension_semantics=("parallel", "arbitrary"),
                                             vmem_limit_bytes=VMEM_LIMIT),
        name="peer_experts_ln2",
    )(h2, u_bf, v_bf, thr, ea, s2, eb, x1, gate2, ln_g, ln_b)


def kernel(x, c, w_ada, b_ada, w_in, attn_sinks, w_pool, pool_scale, w_out, ln1_g, ln1_b,
           w_pq, peer_k1, peer_k2, peer_u, peer_v, ln2_g, ln2_b):
    batch, seq, d = x.shape
    assert d == D_MODEL and seq % MIX_ROWS == 0 and seq % PEER_ROWS == 0
    tokens = batch * seq
    bf = jnp.bfloat16
    half = HEAD_DIM // 2
    freqs = ROPE_THETA ** (-jnp.arange(half, dtype=jnp.float32) / half)
    freq_row = jnp.tile(freqs, LANES // half)[None, :]
    c_pad = jnp.zeros((8, d), jnp.float32).at[:batch].set(c)

    for l in range(w_ada.shape[0]):
        mod = _mod_call(c_pad, w_ada[l], b_ada[l][None, :])[:batch]
        shift1, scale1, gate1, shift2, scale2, gate2 = [
            m[:, None, :] for m in jnp.split(mod, 6, axis=-1)]
        x2d = x.reshape(tokens, d)
        q, k, v, p = _proj_call(x2d, scale1, shift1, w_in[l].astype(bf), freq_row, seq)
        x1, h2 = _mix_call(attn_sinks[l], x2d, q, k, v, p, w_pool[l].astype(bf),
                           pool_scale[l][None, :], w_out[l].astype(bf), gate1, scale2, shift2,
                           ln1_g[l][None, :], ln1_b[l][None, :], seq)
        thr, ea, s2, eb = _route_call(h2, w_pq[l].astype(bf), peer_k1[l].astype(bf),
                                      peer_k2[l].astype(bf))
        out = _peer_call(h2, peer_u[l].astype(bf), peer_v[l].astype(bf), thr, ea, s2, eb, x1,
                         gate2, ln2_g[l][None, :], ln2_b[l][None, :], seq)
        x = out.reshape(batch, seq, d)
    return x
```

```python
import functools
import math

import jax
import jax.numpy as jnp
from jax import lax
from jax.experimental import pallas as pl
from jax.experimental.pallas import tpu as pltpu

D_MODEL = 1024
HEAD_DIM = 64
ATTN_HEADS = 8
ATTN_KV_HEADS = 2
ATTN_GROUP = ATTN_HEADS // ATTN_KV_HEADS
ATTN_WIDTH = ATTN_HEADS * HEAD_DIM
KV_WIDTH = ATTN_KV_HEADS * HEAD_DIM
QK_WIDTH = ATTN_WIDTH + KV_WIDTH
WINDOW = 128
BLOCK = 128
ROPE_THETA = 10000.0
POOL_WINDOWS = (2, 4, 8, 16)
POOL_GROUPS = 4
POOL_WIDTH = D_MODEL - ATTN_WIDTH
POOL_CH = POOL_WIDTH // POOL_GROUPS
POOL_HALO = 16
IN_WIDTH = ATTN_WIDTH + 2 * KV_WIDTH + POOL_WIDTH
PEER_HEADS = 8
PEER_N_KEYS = 128
PEER_N_EXPERTS = PEER_N_KEYS * PEER_N_KEYS
PEER_TOPK = 16
PEER_QDIM = 256
PEER_HALF = PEER_QDIM // 2
DEPTH = 1
DEEPNORM_ALPHA = (2.0 * DEPTH) ** 0.25
LN_EPS = 1e-5
NEG_BIG = -1e30

LANES = 128
WORD_ROWS = 8
GATE_CHUNK = 2

PROJ_ROWS = 512
MIX_ROWS = 512
ROUTE_ROWS = 256
PEER_ROWS = 512
PEER_EXPERT_TILE = 1024
VMEM_LIMIT = 52 * 1024 * 1024

_NT = (((1,), (1,)), ((), ()))
_TN = (((0,), (0,)), ((), ()))


def _layer_norm(z, g, b):
    mu = jnp.mean(z, axis=-1, keepdims=True)
    zc = z - mu
    var = jnp.mean(zc * zc, axis=-1, keepdims=True)
    return zc * lax.rsqrt(var + LN_EPS) * g + b


def _mod_kernel(c_ref, w_ref, b_ref, o_ref):
    c = c_ref[...]
    s = c * jax.nn.sigmoid(c)
    o_ref[...] = jnp.dot(s.astype(jnp.bfloat16), w_ref[...].astype(jnp.bfloat16),
                         preferred_element_type=jnp.float32) + b_ref[...]


def _mod_call(c_pad, w_ada, b_ada):
    n_out = w_ada.shape[1]
    tile = 1024
    return pl.pallas_call(
        _mod_kernel,
        grid=(n_out // tile,),
        in_specs=[
            pl.BlockSpec(c_pad.shape, lambda j: (0, 0)),
            pl.BlockSpec((D_MODEL, tile), lambda j: (0, j)),
            pl.BlockSpec((1, tile), lambda j: (0, j)),
        ],
        out_specs=pl.BlockSpec((c_pad.shape[0], tile), lambda j: (0, j)),
        out_shape=jax.ShapeDtypeStruct((c_pad.shape[0], n_out), jnp.float32),
        compiler_params=pltpu.CompilerParams(dimension_semantics=("parallel",)),
        name="adaln_mod",
    )(c_pad, w_ada, b_ada)


def _proj_kernel(tiles_per_seq, x_ref, scale_ref, shift_ref, w_ref, freq_ref,
                 q_ref, k_ref, v_ref, p_ref):
    i = pl.program_id(0)
    rows = x_ref.shape[0]
    h = x_ref[...] * (1.0 + scale_ref[0]) + shift_ref[0]
    proj = jnp.dot(h.astype(jnp.bfloat16), w_ref[...], preferred_element_type=jnp.float32)

    pos0 = (i % tiles_per_seq) * rows
    pos = (pos0 + lax.broadcasted_iota(jnp.int32, (rows, LANES), 0)).astype(jnp.float32)
    ang = pos * freq_ref[...]
    cos = jnp.cos(ang)
    sin = jnp.sin(ang)
    lane = lax.broadcasted_iota(jnp.int32, (rows, LANES), 1)
    first_half = (lane % HEAD_DIM) < (HEAD_DIM // 2)
    sin_signed = jnp.where(first_half, -sin, sin)

    def rope_chunk(t):
        partner = jnp.where(first_half,
                            pltpu.roll(t, LANES - HEAD_DIM // 2, 1),
                            pltpu.roll(t, HEAD_DIM // 2, 1))
        return t * cos + partner * sin_signed

    q_scale = HEAD_DIM ** -0.5
    for ch in range(ATTN_WIDTH // LANES):
        t = proj[:, ch * LANES:(ch + 1) * LANES]
        q_ref[:, ch * LANES:(ch + 1) * LANES] = (rope_chunk(t) * q_scale).astype(q_ref.dtype)
    k_ref[...] = rope_chunk(proj[:, ATTN_WIDTH:QK_WIDTH]).astype(k_ref.dtype)
    v_ref[...] = proj[:, QK_WIDTH:QK_WIDTH + KV_WIDTH].astype(v_ref.dtype)
    p_ref[...] = proj[:, QK_WIDTH + KV_WIDTH:]


def _proj_call(x2d, scale1, shift1, w_in_bf, freq_row, seq):
    tokens = x2d.shape[0]
    rows = PROJ_ROWS
    tiles_per_seq = seq // rows
    mod_spec = pl.BlockSpec((1, 1, D_MODEL), lambda i: (i // tiles_per_seq, 0, 0))
    return pl.pallas_call(
        functools.partial(_proj_kernel, tiles_per_seq),
        grid=(tokens // rows,),
        in_specs=[
            pl.BlockSpec((rows, D_MODEL), lambda i: (i, 0)),
            mod_spec, mod_spec,
            pl.BlockSpec((D_MODEL, IN_WIDTH), lambda i: (0, 0)),
            pl.BlockSpec((1, LANES), lambda i: (0, 0)),
        ],
        out_specs=[
            pl.BlockSpec((rows, ATTN_WIDTH), lambda i: (i, 0)),
            pl.BlockSpec((rows, KV_WIDTH), lambda i: (i, 0)),
            pl.BlockSpec((rows, KV_WIDTH), lambda i: (i, 0)),
            pl.BlockSpec((rows, POOL_WIDTH), lambda i: (i, 0)),
        ],
        out_shape=[
            jax.ShapeDtypeStruct((tokens, ATTN_WIDTH), jnp.bfloat16),
            jax.ShapeDtypeStruct((tokens, KV_WIDTH), jnp.bfloat16),
            jax.ShapeDtypeStruct((tokens, KV_WIDTH), jnp.bfloat16),
            jax.ShapeDtypeStruct((tokens, POOL_WIDTH), jnp.float32),
        ],
        compiler_params=pltpu.CompilerParams(dimension_semantics=("parallel",),
                                             vmem_limit_bytes=VMEM_LIMIT),
        name="in_proj_rope",
    )(x2d, scale1, shift1, w_in_bf, freq_row)


def _mix_kernel(tiles_per_seq, sinks_ref, x_ref, q_ref, kc_ref, kp_ref, vc_ref, vp_ref,
                pc_ref, pp_ref, wpool_ref, pscale_ref, wout_ref, gate1_ref, scale2_ref,
                shift2_ref, g_ref, b_ref, x1_ref, h2_ref):
    i = pl.program_id(0)
    rows = x_ref.shape[0]
    seq_start = (i % tiles_per_seq) == 0

    kext = jnp.concatenate([kp_ref[...], kc_ref[...]], axis=0)
    vext = jnp.concatenate([vp_ref[...], vc_ref[...]], axis=0)
    r_idx = lax.broadcasted_iota(jnp.int32, (BLOCK, 2 * BLOCK), 0)
    c_idx = lax.broadcasted_iota(jnp.int32, (BLOCK, 2 * BLOCK), 1)
    band = (c_idx > r_idx) & (c_idx <= r_idx + WINDOW)
    first_key = jnp.where(seq_start, BLOCK, 0)
    attn_blocks = []
    for blk in range(rows // BLOCK):
        valid = (band & (c_idx >= first_key)) if blk == 0 else band
        valid4 = jnp.concatenate([valid] * ATTN_GROUP, axis=0)
        kcat = kext[blk * BLOCK:(blk + 2) * BLOCK]
        vcat = vext[blk * BLOCK:(blk + 2) * BLOCK]
        q_blk = q_ref[blk * BLOCK:(blk + 1) * BLOCK, :]
        head_out = []
        for g in range(ATTN_KV_HEADS):
            heads = range(g * ATTN_GROUP, (g + 1) * ATTN_GROUP)
            q4 = jnp.concatenate([q_blk[:, h * HEAD_DIM:(h + 1) * HEAD_DIM] for h in heads], axis=0)
            s = lax.dot_general(q4, kcat[:, g * HEAD_DIM:(g + 1) * HEAD_DIM], _NT,
                                preferred_element_type=jnp.float32)
            s = jnp.where(valid4, s, NEG_BIG)
            sink = jnp.concatenate(
                [jnp.full((BLOCK, 1), sinks_ref[h], jnp.float32) for h in heads], axis=0)
            m = jnp.maximum(jnp.max(s, axis=-1, keepdims=True), sink)
            p = jnp.exp(s - m)
            denom = jnp.sum(p, axis=-1, keepdims=True) + jnp.exp(sink - m)
            o = jnp.dot(p.astype(jnp.bfloat16), vcat[:, g * HEAD_DIM:(g + 1) * HEAD_DIM],
                        preferred_element_type=jnp.float32) / denom
            head_out += [o[k * BLOCK:(k + 1) * BLOCK] for k in range(ATTN_GROUP)]
        attn_blocks.append(jnp.concatenate(head_out, axis=1))
    attn = jnp.concatenate(attn_blocks, axis=0)

    halo = jnp.where(seq_start, 0.0, pp_ref[...])
    pext = jnp.concatenate([halo, pc_ref[...]], axis=0)
    pos = ((i % tiles_per_seq) * rows
           + lax.broadcasted_iota(jnp.int32, (rows, 1), 0))
    pool_parts = []
    for gi, w in enumerate(POOL_WINDOWS):
        cols = slice(gi * POOL_CH, (gi + 1) * POOL_CH)
        tot = pext[POOL_HALO:POOL_HALO + rows, cols]
        for d in range(1, w):
            tot = tot + pext[POOL_HALO - d:POOL_HALO - d + rows, cols]
        cnt = jnp.minimum(pos + 1, w).astype(jnp.float32)
        pooled = tot / cnt - pext[POOL_HALO:POOL_HALO + rows, cols]
        pool_parts.append(jnp.dot(pooled.astype(jnp.bfloat16), wpool_ref[gi],
                                  preferred_element_type=jnp.float32))
    pool = jnp.concatenate(pool_parts, axis=1) * pscale_ref[...]

    mix = (jnp.dot(attn.astype(jnp.bfloat16), wout_ref[:ATTN_WIDTH, :],
                   preferred_element_type=jnp.float32)
           + jnp.dot(pool.astype(jnp.bfloat16), wout_ref[ATTN_WIDTH:, :],
                     preferred_element_type=jnp.float32))
    z = DEEPNORM_ALPHA * x_ref[...] + (1.0 + gate1_ref[0]) * mix
    x1 = _layer_norm(z, g_ref[...], b_ref[...])
    x1_ref[...] = x1
    h2_ref[...] = (x1 * (1.0 + scale2_ref[0]) + shift2_ref[0]).astype(h2_ref.dtype)


def _mix_call(sinks, x2d, q, k, v, p, w_pool_bf, pool_scale, w_out_bf, gate1, scale2, shift2,
              ln_g, ln_b, seq):
    tokens = x2d.shape[0]
    rows = MIX_ROWS
    tiles_per_seq = seq // rows
    blocks_per_tile = rows // BLOCK
    halos_per_tile = rows // POOL_HALO
    mod_spec = pl.BlockSpec((1, 1, D_MODEL), lambda i, s: (i // tiles_per_seq, 0, 0))
    row_spec = lambda width: pl.BlockSpec((rows, width), lambda i, s: (i, 0))
    prev_block = lambda i, s: (jnp.maximum(i * blocks_per_tile - 1, 0), 0)
    prev_halo = lambda i, s: (jnp.maximum(i * halos_per_tile - 1, 0), 0)
    full = lambda shape: pl.BlockSpec(shape, lambda i, s: (0,) * len(shape))
    grid_spec = pltpu.PrefetchScalarGridSpec(
        num_scalar_prefetch=1,
        grid=(tokens // rows,),
        in_specs=[
            row_spec(D_MODEL),
            row_spec(ATTN_WIDTH),
            row_spec(KV_WIDTH),
            pl.BlockSpec((BLOCK, KV_WIDTH), prev_block),
            row_spec(KV_WIDTH),
            pl.BlockSpec((BLOCK, KV_WIDTH), prev_block),
            row_spec(POOL_WIDTH),
            pl.BlockSpec((POOL_HALO, POOL_WIDTH), prev_halo),
            full((POOL_GROUPS, POOL_CH, POOL_CH)),
            full((1, POOL_WIDTH)),
            full((D_MODEL, D_MODEL)),
            mod_spec, mod_spec, mod_spec,
            full((1, D_MODEL)), full((1, D_MODEL)),
        ],
        out_specs=[row_spec(D_MODEL), row_spec(D_MODEL)],
    )
    return pl.pallas_call(
        functools.partial(_mix_kernel, tiles_per_seq),
        grid_spec=grid_spec,
        out_shape=[
            jax.ShapeDtypeStruct((tokens, D_MODEL), jnp.float32),
            jax.ShapeDtypeStruct((tokens, D_MODEL), jnp.bfloat16),
        ],
        compiler_params=pltpu.CompilerParams(dimension_semantics=("parallel",),
                                             vmem_limit_bytes=VMEM_LIMIT),
        name="mixers_ln1",
    )(sinks, x2d, q, k, k, v, v, p, p, w_pool_bf, pool_scale, w_out_bf, gate1, scale2, shift2,
      ln_g, ln_b)


def _top_rows(s, count, with_rank=False):
    out = []
    rank = jnp.full(s.shape, float(count), jnp.float32)
    for k in range(count):
        m = jnp.max(s, axis=0, keepdims=True)
        out.append(m)
        hit = s >= m
        if with_rank:
            rank = jnp.where(hit, float(k), rank)
        s = jnp.where(hit, -jnp.inf, s)
    return (out, rank) if with_rank else out


def _dup_bf16_word(x):
    hi = pltpu.bitcast(x.astype(jnp.bfloat16).astype(jnp.float32), jnp.uint32)
    return hi | (hi >> 16)


def _pair_bf16_word(x):
    half = x.shape[0] // 2
    bits = pltpu.bitcast(x.astype(jnp.bfloat16).astype(jnp.float32), jnp.uint32)
    return bits[half:] | (bits[:half] >> 16)


def _peer_row_order(table):
    n, d = table.shape
    half = PEER_N_KEYS // 2
    t = table.reshape(n // PEER_N_KEYS, 2, half, d)
    return jnp.swapaxes(t, 1, 2).reshape(n, d)


def _route_kernel(h2_ref, wpq_ref, k1_ref, k2_ref, cnt_ref, ea_ref, rank_ref, eb_ref):
    q = jnp.dot(h2_ref[...], wpq_ref[...], preferred_element_type=jnp.float32)
    q = q.astype(jnp.bfloat16)
    half_k = PEER_TOPK // 2
    for h in range(PEER_HEADS):
        qa = q[:, h * PEER_QDIM:h * PEER_QDIM + PEER_HALF]
        qb = q[:, h * PEER_QDIM + PEER_HALF:(h + 1) * PEER_QDIM]
        s1 = lax.dot_general(k1_ref[...], qa, _NT, preferred_element_type=jnp.float32)
        s2 = lax.dot_general(k2_ref[...], qb, _NT, preferred_element_type=jnp.float32)
        a = _top_rows(s1, PEER_TOPK)
        b, rank2 = _top_rows(s2, PEER_TOPK, with_rank=True)
        b_lo = jnp.concatenate(b[:half_k], axis=0)
        cands = [a[r] + b_lo for r in range(half_k)]
        cands.append(a[0] + jnp.concatenate(b[half_k:], axis=0))
        cands.append(jnp.concatenate(a[half_k:], axis=0) + b[0])
        cand = jnp.concatenate(cands, axis=0)
        tau = _top_rows(cand, PEER_TOPK)[-1]
        top = a[0] + b[0]
        z = jnp.sum(jnp.where(cand >= tau, jnp.exp(cand - top), 0.0), axis=0, keepdims=True)
        cnt = jnp.zeros_like(s1)
        for k in range(PEER_TOPK):
            cnt = cnt + jnp.where(s1 + b[k] >= tau, 1.0, 0.0)
        outs = ((cnt_ref, _dup_bf16_word(cnt)),
                (ea_ref, _dup_bf16_word(jnp.exp(s1 - a[0]))),
                (rank_ref, _pair_bf16_word(rank2)),
                (eb_ref, _pair_bf16_word(jnp.exp(s2 - b[0]) / z)))
        for ref, val in outs:
            for tb in range(val.shape[1] // LANES):
                ref[h, tb] = val[:, tb * LANES:(tb + 1) * LANES]


def _route_call(h2, w_pq_bf, k1_bf, k2_bf):
    tokens = h2.shape[0]
    rows = ROUTE_ROWS

    def table(n_rows):
        return (jax.ShapeDtypeStruct((PEER_HEADS, tokens // LANES, n_rows, LANES), jnp.uint32),
                pl.BlockSpec((PEER_HEADS, rows // LANES, n_rows, LANES), lambda i: (0, i, 0, 0)))

    tables = [table(PEER_N_KEYS), table(PEER_N_KEYS), table(PEER_N_KEYS // 2),
              table(PEER_N_KEYS // 2)]
    return pl.pallas_call(
        _route_kernel,
        grid=(tokens // rows,),
        in_specs=[
            pl.BlockSpec((rows, D_MODEL), lambda i: (i, 0)),
            pl.BlockSpec((D_MODEL, PEER_HEADS * PEER_QDIM), lambda i: (0, 0)),
            pl.BlockSpec((PEER_N_KEYS, PEER_HALF), lambda i: (0, 0)),
            pl.BlockSpec((PEER_N_KEYS, PEER_HALF), lambda i: (0, 0)),
        ],
        out_specs=[spec for _, spec in tables],
        out_shape=[sds for sds, _ in tables],
        compiler_params=pltpu.CompilerParams(dimension_semantics=("parallel",),
                                             vmem_limit_bytes=VMEM_LIMIT),
        name="peer_route",
    )(h2, w_pq_bf, k1_bf, k2_bf)


def _peer_kernel(h2_ref, u_ref, v_ref, cnt_ref, ea_ref, rank_ref, eb_ref, x1_ref, gate2_ref,
                 g_ref, b_ref, o_ref, acc_ref, hid_ref, act_ref):
    j = pl.program_id(1)
    rows = h2_ref.shape[0]
    rows_i = u_ref.shape[0] // PEER_N_KEYS

    @pl.when(j == 0)
    def _():
        acc_ref[...] = jnp.zeros_like(acc_ref)

    hid_ref[...] = lax.dot_general(u_ref[...], h2_ref[...], _NT,
                                   preferred_element_type=jnp.float32)
    sqrt_half = math.sqrt(0.5)
    bf = jnp.bfloat16
    zero = jnp.zeros((2 * WORD_ROWS, LANES), bf)
    words_per_i = PEER_N_KEYS // 2

    def lane_row(ref, h, tb, il):
        word = jnp.broadcast_to(ref[h, tb, il:il + 1, :], (WORD_ROWS, LANES))
        return pltpu.bitcast(word, bf)

    for tb in range(rows // LANES):
        tok = slice(tb * LANES, (tb + 1) * LANES)
        for jc in range(words_per_i // (GATE_CHUNK * WORD_ROWS)):
            subs = [slice((jc * GATE_CHUNK + k) * WORD_ROWS, (jc * GATE_CHUNK + k + 1) * WORD_ROWS)
                    for k in range(GATE_CHUNK)]
            gates = [[None] * GATE_CHUNK for _ in range(rows_i)]
            for h in range(PEER_HEADS):
                rank = [pltpu.bitcast(rank_ref[h, tb, s, :], bf) for s in subs]
                eb = [pltpu.bitcast(eb_ref[h, tb, s, :], bf) for s in subs]
                for il in range(rows_i):
                    cnt = lane_row(cnt_ref, h, tb, il)
                    ea = lane_row(ea_ref, h, tb, il)
                    for k in range(GATE_CHUNK):
                        term = jnp.where(rank[k] < cnt, eb[k], zero) * ea
                        gates[il][k] = term if gates[il][k] is None else gates[il][k] + term
            for il in range(rows_i):
                for k, s in enumerate(subs):
                    w_rows = slice(il * words_per_i + s.start, il * words_per_i + s.stop)
                    hid = hid_ref[2 * w_rows.start:2 * w_rows.stop, tok]
                    gelu = 0.5 * hid * (1.0 + lax.erf(hid * sqrt_half))
                    act_ref[w_rows, tok] = pltpu.bitcast(gelu.astype(bf) * gates[il][k],
                                                         jnp.uint32)

    acc_ref[...] += lax.dot_general(pltpu.bitcast(act_ref[...], bf), v_ref[...], _TN,
                                    preferred_element_type=jnp.float32)

    @pl.when(j == pl.num_programs(1) - 1)
    def _():
        z = DEEPNORM_ALPHA * x1_ref[...] + (1.0 + gate2_ref[0]) * acc_ref[...]
        o_ref[...] = _layer_norm(z, g_ref[...], b_ref[...])


def _peer_call(h2, u_bf, v_bf, cnt, ea, rank, eb, x1, gate2, ln_g, ln_b, seq):
    tokens = h2.shape[0]
    rows = PEER_ROWS
    et = PEER_EXPERT_TILE
    tiles_per_seq = seq // rows
    rows_i = et // PEER_N_KEYS
    sel_spec = pl.BlockSpec((PEER_HEADS, rows // LANES, rows_i, LANES), lambda i, j: (0, i, j, 0))
    key_spec = pl.BlockSpec((PEER_HEADS, rows // LANES, PEER_N_KEYS // 2, LANES),
                            lambda i, j: (0, i, 0, 0))
    return pl.pallas_call(
        _peer_kernel,
        grid=(tokens // rows, PEER_N_EXPERTS // et),
        in_specs=[
            pl.BlockSpec((rows, D_MODEL), lambda i, j: (i, 0)),
            pl.BlockSpec((et, D_MODEL), lambda i, j: (j, 0)),
            pl.BlockSpec((et, D_MODEL), lambda i, j: (j, 0)),
            sel_spec, sel_spec,
            key_spec, key_spec,
            pl.BlockSpec((rows, D_MODEL), lambda i, j: (i, 0)),
            pl.BlockSpec((1, 1, D_MODEL), lambda i, j: (i // tiles_per_seq, 0, 0)),
            pl.BlockSpec((1, D_MODEL), lambda i, j: (0, 0)),
            pl.BlockSpec((1, D_MODEL), lambda i, j: (0, 0)),
        ],
        out_specs=pl.BlockSpec((rows, D_MODEL), lambda i, j: (i, 0)),
        out_shape=jax.ShapeDtypeStruct((tokens, D_MODEL), jnp.float32),
        scratch_shapes=[
            pltpu.VMEM((rows, D_MODEL), jnp.float32),
            pltpu.VMEM((et, rows), jnp.float32),
            pltpu.VMEM((et // 2, rows), jnp.uint32),
        ],
        compiler_params=pltpu.CompilerParams(dimension_semantics=("parallel", "arbitrary"),
                                             vmem_limit_bytes=VMEM_LIMIT),
        name="peer_experts_ln2",
    )(h2, u_bf, v_bf, cnt, ea, rank, eb, x1, gate2, ln_g, ln_b)


def kernel(x, c, w_ada, b_ada, w_in, attn_sinks, w_pool, pool_scale, w_out, ln1_g, ln1_b,
           w_pq, peer_k1, peer_k2, peer_u, peer_v, ln2_g, ln2_b):
    batch, seq, d = x.shape
    assert d == D_MODEL and seq % MIX_ROWS == 0 and seq % PEER_ROWS == 0
    tokens = batch * seq
    bf = jnp.bfloat16
    half = HEAD_DIM // 2
    freqs = ROPE_THETA ** (-jnp.arange(half, dtype=jnp.float32) / half)
    freq_row = jnp.tile(freqs, LANES // half)[None, :]
    c_pad = jnp.zeros((8, d), jnp.float32).at[:batch].set(c)

    for l in range(w_ada.shape[0]):
        mod = _mod_call(c_pad, w_ada[l], b_ada[l][None, :])[:batch]
        shift1, scale1, gate1, shift2, scale2, gate2 = [
            m[:, None, :] for m in jnp.split(mod, 6, axis=-1)]
        x2d = x.reshape(tokens, d)
        q, k, v, p = _proj_call(x2d, scale1, shift1, w_in[l].astype(bf), freq_row, seq)
        x1, h2 = _mix_call(attn_sinks[l], x2d, q, k, v, p, w_pool[l].astype(bf),
                           pool_scale[l][None, :], w_out[l].astype(bf), gate1, scale2, shift2,
                           ln1_g[l][None, :], ln1_b[l][None, :], seq)
        cnt, ea, rank, eb = _route_call(h2, w_pq[l].astype(bf), peer_k1[l].astype(bf),
                                      peer_k2[l].astype(bf))
        out = _peer_call(h2, _peer_row_order(peer_u[l].astype(bf)),
                         _peer_row_order(peer_v[l].astype(bf)), cnt, ea, rank, eb, x1,
                         gate2, ln2_g[l][None, :], ln2_b[l][None, :], seq)
        x = out.reshape(batch, seq, d)
    return x
```
